```python
import math
import jax
import jax.numpy as jnp
from jax import lax
import numpy as np

D_MODEL = 2048
BATCH = 4
SEQ = 4096
DEPTH = 2

MIX_W = D_MODEL
HG_W = D_MODEL // 4
SSM_W = D_MODEL // 2
ATT_W = MIX_W - HG_W - SSM_W

HG_DK = 128
HG_HEADS = HG_W // HG_DK
HG_CHUNK = 64

SSM_HEADDIM = 64
SSM_HEADS = SSM_W // SSM_HEADDIM
SSM_GROUPS = 2
SSM_STATE = 128
SSM_CONV = 4
SSM_CHUNK = 128
SSM_CONV_CH = SSM_W + 2 * SSM_GROUPS * SSM_STATE

ATT_HEAD_DIM = 128
ATT_HEADS = ATT_W // ATT_HEAD_DIM
MOBA_BLOCK = 256
MOBA_TOPK = 3
MOBA_QCHUNK = 32

D_FF = 4 * D_MODEL
DEEPNORM_ALPHA = (2 * DEPTH) ** 0.25
DEEPNORM_BETA = (8 * DEPTH) ** -0.25
NORM_EPS = 1e-5
MASK_NEG = -1e30
GATE_FLOOR = 1e-30

HG_OFF = 0
SSM_Z_OFF = HG_OFF + 4 * HG_W
SSM_XBC_OFF = SSM_Z_OFF + SSM_W
SSM_DT_OFF = SSM_XBC_OFF + SSM_CONV_CH
ATT_OFF = SSM_DT_OFF + SSM_HEADS
N_IN = ATT_OFF + 3 * ATT_W

kernel_name = 'hymba_style_hgrn2_mamba2_moba_deepnorm'


def layer_norm(x, g, b):
    xf = x.astype(jnp.float32)
    mu = jnp.mean(xf, -1, keepdims=True)
    var = jnp.mean(jnp.square(xf - mu), -1, keepdims=True)
    y = (xf - mu) * lax.rsqrt(var + NORM_EPS) * g.astype(jnp.float32) + b.astype(jnp.float32)
    return y.astype(x.dtype)


def rms_norm_f32(x, w):
    return x * lax.rsqrt(jnp.mean(jnp.square(x), -1, keepdims=True) + NORM_EPS) * w.astype(jnp.float32)


def segsum_from_cumsum(cs):
    T = cs.shape[-1]
    diff = cs[..., :, None] - cs[..., None, :]
    mask = jnp.tril(jnp.ones((T, T), dtype=bool))
    return jnp.where(mask, diff, MASK_NEG)


def alibi_slopes(n_heads):
    return jnp.asarray([2.0 ** (-8.0 * (h + 1) / n_heads) for h in range(n_heads)], dtype=jnp.float32)


def hgrn2_mixer(q, f, v, g, lb, norm_w):
    Bsz, S, _ = q.shape
    C = HG_CHUNK
    NC = S // C
    f32 = jnp.float32
    lb = lb.astype(f32)
    fg = lb + (1.0 - lb) * jax.nn.sigmoid(f.astype(f32))
    log_f = jnp.log(jnp.maximum(fg, GATE_FLOOR))
    k = 1.0 - fg
    qf = jax.nn.silu(q.astype(f32))

    def to_chunks(t):
        return t.reshape(Bsz, NC, C, HG_HEADS, HG_DK).transpose(1, 0, 3, 2, 4)

    causal = jnp.tril(jnp.ones((C, C), dtype=bool))

    def step(state, inp):
        qc, kc, vc, gc = inp
        b = jnp.cumsum(gc, axis=2)
        diff = b[:, :, :, None, :] - b[:, :, None, :, :]
        decay = jnp.exp(jnp.where(causal[:, :, None], diff, MASK_NEG))
        attn = jnp.einsum('bhtk,bhsk,bhtsk->bhts', qc, kc, decay)
        o = jnp.einsum('bhts,bhsv->bhtv', attn, vc) + jnp.einsum('bhtk,bhkv->bhtv', qc * jnp.exp(b), state)
        b_last = b[:, :, -1:, :]
        state = jnp.exp(b_last[:, :, 0])[..., None] * state + jnp.einsum('bhsk,bhsv->bhkv', kc * jnp.exp(b_last - b), vc)
        return state, o

    state0 = jnp.zeros((Bsz, HG_HEADS, HG_DK, HG_DK), f32)
    _, o = lax.scan(step, state0, (to_chunks(qf), to_chunks(k), to_chunks(v.astype(f32)), to_chunks(log_f)))
    o = o.transpose(1, 0, 3, 2, 4).reshape(Bsz, S, HG_HEADS, HG_DK)
    o = rms_norm_f32(o, norm_w.reshape(HG_HEADS, HG_DK)).reshape(Bsz, S, HG_W)
    o = o * jax.nn.silu(g.astype(f32))
    return o.astype(q.dtype)


def causal_depthwise_conv(u, w, b):
    y = lax.conv_general_dilated(u, w[:, None, :].astype(u.dtype), window_strides=(1,),
                                 padding=[(SSM_CONV - 1, 0)], dimension_numbers=('NWC', 'WIO', 'NWC'),
                                 feature_group_count=u.shape[-1])
    return y + b.astype(u.dtype)


def mamba2_mixer(z, xbc, dt, conv_w, conv_b, dt_bias, a_log, d_skip, norm_w):
    Bsz, S, _ = z.shape
    f32 = jnp.float32
    G, E, P, N, L = SSM_GROUPS, SSM_HEADS // SSM_GROUPS, SSM_HEADDIM, SSM_STATE, SSM_CHUNK
    NC = S // L
    xbc = jax.nn.silu(causal_depthwise_conv(xbc, conv_w, conv_b)).astype(f32)
    xs = xbc[..., :SSM_W]
    Bm = xbc[..., SSM_W:SSM_W + G * N].reshape(Bsz, NC, L, G, N)
    Cm = xbc[..., SSM_W + G * N:].reshape(Bsz, NC, L, G, N)
    dt = jax.nn.softplus(dt.astype(f32) + dt_bias.astype(f32))
    A = -jnp.exp(a_log.astype(f32))
    xh = xs.reshape(Bsz, S, SSM_HEADS, P)
    X = (xh * dt[..., None]).reshape(Bsz, NC, L, G, E, P)
    dA = (dt * A).reshape(Bsz, NC, L, G, E).transpose(0, 3, 4, 1, 2)
    Acum = jnp.cumsum(dA, axis=-1)
    Lmat = jnp.exp(segsum_from_cumsum(Acum))
    CB = jnp.einsum('bclgn,bcsgn->bgcls', Cm, Bm)
    y_diag = jnp.einsum('bgcls,bgecls,bcsgep->bclgep', CB, Lmat, X)
    decay_states = jnp.exp(Acum[..., -1:] - Acum)
    states = jnp.einsum('bcsgn,bgecs,bcsgep->bcgepn', Bm, decay_states, X)
    states = jnp.concatenate([jnp.zeros_like(states[:, :1]), states], axis=1)
    chunk_cum = jnp.cumsum(jnp.pad(Acum[..., -1], ((0, 0), (0, 0), (0, 0), (1, 0))), axis=-1)
    decay_chunk = jnp.exp(segsum_from_cumsum(chunk_cum))
    prev_states = jnp.einsum('bgezc,bcgepn->bzgepn', decay_chunk, states)[:, :-1]
    y_off = jnp.einsum('bclgn,bcgepn,bgecl->bclgep', Cm, prev_states, jnp.exp(Acum))
    y = (y_diag + y_off).reshape(Bsz, S, SSM_HEADS, P) + d_skip.astype(f32)[:, None] * xh
    y = y.reshape(Bsz, S, SSM_W) * jax.nn.silu(z.astype(f32))
    y = rms_norm_f32(y.reshape(Bsz, S, G, SSM_W // G), norm_w.reshape(G, SSM_W // G)).reshape(Bsz, S, SSM_W)
    return y.astype(z.dtype)


def moba_attention(q, k, v):
    Bsz, S, _ = q.shape
    f32 = jnp.float32
    H, Dh, BS, QC = ATT_HEADS, ATT_HEAD_DIM, MOBA_BLOCK, MOBA_QCHUNK
    S_pad = -(-S // BS) * BS
    NB = S_pad // BS
    kk = min(MOBA_TOPK, NB)
    NQ = S_pad // QC

    def heads(t):
        t = t.reshape(Bsz, S, H, Dh).transpose(0, 2, 1, 3)
        return jnp.pad(t, ((0, 0), (0, 0), (0, S_pad - S), (0, 0)))

    q, k, v = heads(q), heads(k), heads(v)
    kb = k.reshape(Bsz, H, NB, BS, Dh)
    vb = v.reshape(Bsz, H, NB, BS, Dh)
    kmean = jnp.mean(kb.astype(f32), axis=3)
    gate = jnp.einsum('bhsd,bhnd->bhsn', q.astype(f32), kmean)
    qblk = jnp.arange(S_pad) // BS
    fully_past = jnp.arange(NB)[None, :] < qblk[:, None]
    gate = jnp.where(fully_past, gate, MASK_NEG)
    _, sel = lax.top_k(gate, kk)
    slopes = alibi_slopes(H)[None, :, None, None]
    scale = Dh ** -0.5
    b_idx = jnp.arange(Bsz)[:, None, None, None]
    h_idx = jnp.arange(H)[None, :, None, None]
    q_chunks = q.reshape(Bsz, H, NQ, QC, Dh).transpose(2, 0, 1, 3, 4)
    sel_chunks = sel.reshape(Bsz, H, NQ, QC, kk).transpose(2, 0, 1, 3, 4)

    def chunk_attend(args):
        ci, qc, sc = args
        t = ci * QC + jnp.arange(QC)
        blk = (ci * QC) // BS
        k_own = lax.dynamic_index_in_dim(kb, blk, axis=2, keepdims=False)
        v_own = lax.dynamic_index_in_dim(vb, blk, axis=2, keepdims=False)
        dist_own = t[:, None] - (blk * BS + jnp.arange(BS))[None, :]
        s_own = jnp.einsum('bhqd,bhjd->bhqj', qc, k_own).astype(f32) * scale - slopes * dist_own.astype(f32)
        s_own = jnp.where(dist_own >= 0, s_own, MASK_NEG)
        k_sel = kb[b_idx, h_idx, sc]
        v_sel = vb[b_idx, h_idx, sc]
        dist_sel = t[None, None, :, None, None] - (sc[..., None] * BS + jnp.arange(BS))
        s_sel = jnp.einsum('bhqd,bhqrjd->bhqrj', qc, k_sel).astype(f32) * scale - slopes[..., None] * dist_sel.astype(f32)
        valid = jnp.arange(kk)[None, :] < (t // BS)[:, None]
        s_sel = jnp.where(valid[:, :, None], s_sel, MASK_NEG)
        scores = jnp.concatenate([s_own, s_sel.reshape(Bsz, H, QC, kk * BS)], axis=-1)
        p = jax.nn.softmax(scores, axis=-1).astype(v.dtype)
        p_own = p[..., :BS]
        p_sel = p[..., BS:].reshape(Bsz, H, QC, kk, BS)
        return jnp.einsum('bhqj,bhjd->bhqd', p_own, v_own) + jnp.einsum('bhqrj,bhqrjd->bhqd', p_sel, v_sel)

    out = lax.map(chunk_attend, (jnp.arange(NQ), q_chunks, sel_chunks))
    out = out.transpose(1, 0, 3, 2, 4).reshape(Bsz, S_pad, H * Dh)
    return out[:, :S]


def setup_inputs(seed: int = 0) -> dict:
    key = jax.random.key(seed)
    ks = jax.random.split(key, 18)
    f32 = jnp.float32
    Lr = DEPTH
    nrm = lambda k, shp: jax.random.normal(k, shp, f32)
    x = nrm(ks[0], (BATCH, SEQ, D_MODEL))
    w_in = nrm(ks[1], (Lr, D_MODEL, N_IN)) * D_MODEL ** -0.5
    hg_lower_bounds = 0.1 * nrm(ks[2], (Lr, HG_W))
    hg_norm_w = 1.0 + 0.02 * nrm(ks[3], (Lr, HG_W))
    ssm_conv_w = nrm(ks[4], (Lr, SSM_CONV, SSM_CONV_CH)) * SSM_CONV ** -0.5
    ssm_conv_b = 0.02 * nrm(ks[5], (Lr, SSM_CONV_CH))
    dt0 = jnp.exp(jax.random.uniform(ks[6], (Lr, SSM_HEADS), f32, math.log(1e-3), math.log(1e-1)))
    ssm_dt_bias = dt0 + jnp.log(-jnp.expm1(-dt0))
    ssm_a_log = jnp.log(jax.random.uniform(ks[7], (Lr, SSM_HEADS), f32, 1.0, 16.0))
    ssm_d = 1.0 + 0.1 * nrm(ks[8], (Lr, SSM_HEADS))
    ssm_norm_w = 1.0 + 0.02 * nrm(ks[9], (Lr, SSM_W))
    w_out = nrm(ks[10], (Lr, MIX_W, D_MODEL)) * (MIX_W ** -0.5) * DEEPNORM_BETA
    ln1_g = 1.0 + 0.02 * nrm(ks[11], (Lr, D_MODEL))
    ln1_b = 0.02 * nrm(ks[12], (Lr, D_MODEL))
    w_mlp_in = nrm(ks[13], (Lr, D_MODEL, D_FF)) * D_MODEL ** -0.5
    w_mlp_out = nrm(ks[14], (Lr, D_FF, D_MODEL)) * (D_FF ** -0.5) * DEEPNORM_BETA
    ln2_g = 1.0 + 0.02 * nrm(ks[15], (Lr, D_MODEL))
    ln2_b = 0.02 * nrm(ks[16], (Lr, D_MODEL))
    return {'x': x, 'w_in': w_in, 'hg_lower_bounds': hg_lower_bounds, 'hg_norm_w': hg_norm_w,
            'ssm_conv_w': ssm_conv_w, 'ssm_conv_b': ssm_conv_b, 'ssm_dt_bias': ssm_dt_bias,
            'ssm_a_log': ssm_a_log, 'ssm_d': ssm_d, 'ssm_norm_w': ssm_norm_w, 'w_out': w_out,
            'ln1_g': ln1_g, 'ln1_b': ln1_b, 'w_mlp_in': w_mlp_in, 'w_mlp_out': w_mlp_out,
            'ln2_g': ln2_g, 'ln2_b': ln2_b}


def reference(x, w_in, hg_lower_bounds, hg_norm_w, ssm_conv_w, ssm_conv_b, ssm_dt_bias, ssm_a_log,
              ssm_d, ssm_norm_w, w_out, ln1_g, ln1_b, w_mlp_in, w_mlp_out, ln2_g, ln2_b):
    lb_soft = jax.nn.softmax(hg_lower_bounds.astype(jnp.float32), axis=0)
    lower_bounds = jnp.cumsum(lb_soft, axis=0) - lb_soft[0]
    for l in range(DEPTH):
        proj = jnp.einsum('bsd,dn->bsn', x, w_in[l])
        hq = proj[..., HG_OFF:HG_OFF + HG_W]
        hf = proj[..., HG_OFF + HG_W:HG_OFF + 2 * HG_W]
        hi = proj[..., HG_OFF + 2 * HG_W:HG_OFF + 3 * HG_W]
        hg = proj[..., HG_OFF + 3 * HG_W:SSM_Z_OFF]
        sz = proj[..., SSM_Z_OFF:SSM_XBC_OFF]
        sxbc = proj[..., SSM_XBC_OFF:SSM_DT_OFF]
        sdt = proj[..., SSM_DT_OFF:ATT_OFF]
        aq = proj[..., ATT_OFF:ATT_OFF + ATT_W]
        ak = proj[..., ATT_OFF + ATT_W:ATT_OFF + 2 * ATT_W]
        av = proj[..., ATT_OFF + 2 * ATT_W:N_IN]
        y_hg = hgrn2_mixer(hq, hf, hi, hg, lower_bounds[l], hg_norm_w[l])
        y_ssm = mamba2_mixer(sz, sxbc, sdt, ssm_conv_w[l], ssm_conv_b[l], ssm_dt_bias[l], ssm_a_log[l],
                             ssm_d[l], ssm_norm_w[l])
        y_att = moba_attention(aq, ak, av)
        mix = jnp.einsum('bsm,md->bsd', jnp.concatenate([y_hg, y_ssm, y_att], axis=-1), w_out[l])
        h = layer_norm(DEEPNORM_ALPHA * x + mix, ln1_g[l], ln1_b[l])
        ff = jnp.einsum('bsf,fd->bsd', jnp.square(jax.nn.relu(jnp.einsum('bsd,df->bsf', h, w_mlp_in[l]))), w_mlp_out[l])
        x = layer_norm(DEEPNORM_ALPHA * h + ff, ln2_g[l], ln2_b[l])
    return x
```

```python
import functools

import numpy as np
import jax
import jax.numpy as jnp
from jax import lax
from jax.experimental import pallas as pl
from jax.experimental.pallas import tpu as pltpu

F32 = jnp.float32
BF16 = jnp.bfloat16

LANES_V7X = 128
VMEM_LIMIT_BYTES_V7X = 60000 * 1024

D_MODEL = 2048
DEPTH = 2
HG_W = D_MODEL // 4
SSM_W = D_MODEL // 2
ATT_W = D_MODEL - HG_W - SSM_W
HG_DK = 128
HG_HEADS = HG_W // HG_DK
SSM_HEADDIM = 64
SSM_HEADS = SSM_W // SSM_HEADDIM
SSM_GROUPS = 2
SSM_STATE = 128
SSM_CONV = 4
SSM_CONV_CH = SSM_W + 2 * SSM_GROUPS * SSM_STATE
ATT_HEAD_DIM = 128
ATT_HEADS = ATT_W // ATT_HEAD_DIM
MOBA_BLOCK = 256
MOBA_TOPK = 3
D_FF = 4 * D_MODEL
DEEPNORM_ALPHA = (2 * DEPTH) ** 0.25
NORM_EPS = 1e-5
MASK_NEG = -1e30
GATE_FLOOR = 1e-30

SSM_Z_OFF = 4 * HG_W
SSM_XBC_OFF = SSM_Z_OFF + SSM_W
SSM_DT_OFF = SSM_XBC_OFF + SSM_CONV_CH
ATT_OFF = SSM_DT_OFF + SSM_HEADS
N_IN = ATT_OFF + 3 * ATT_W

P_Z = 4 * HG_W
P_XBC = P_Z + SSM_W
P_ATT = P_XBC + SSM_CONV_CH
P_DT = P_ATT + 3 * ATT_W
N_PROJ = P_DT + LANES_V7X

CHUNK = 128

_NT = (((1,), (1,)), ((), ()))


def _dot(a, b):
    return jnp.dot(a, b, preferred_element_type=F32)


def _dot_nt(a, b):
    return lax.dot_general(a, b, _NT, preferred_element_type=F32)


def _split2(x):
    hi = x.astype(BF16)
    lo = (x - hi.astype(F32)).astype(BF16)
    return hi, lo


def _dot_exact_rhs(m, x):
    hi, lo = _split2(x)
    return _dot(m, hi) + _dot(m, lo)


def _dot_exact_lhs(x, m):
    hi, lo = _split2(x)
    return _dot(hi, m) + _dot(lo, m)


def _sigmoid(x):
    return 1.0 / (1.0 + jnp.exp(-x))


def _silu(x):
    return x * _sigmoid(x)


def _softplus(x):
    return jnp.maximum(x, 0.0) + jnp.log(1.0 + jnp.exp(-jnp.abs(x)))


def _params(sem):
    return pltpu.CompilerParams(dimension_semantics=sem, vmem_limit_bytes=VMEM_LIMIT_BYTES_V7X)


def _inproj_kernel(x_ref, w_ref, o_ref, xb_ref):
    @pl.when(pl.program_id(1) == 0)
    def _():
        xb_ref[...] = x_ref[...].astype(BF16)

    o_ref[...] = _dot(xb_ref[...], w_ref[...])


def _in_proj(x2, w, tm, tn):
    T, D = x2.shape
    N = w.shape[1]
    return pl.pallas_call(
        _inproj_kernel,
        grid=(T // tm, N // tn),
        in_specs=[pl.BlockSpec((tm, D), lambda i, j: (i, 0)),
                  pl.BlockSpec((D, tn), lambda i, j: (0, j))],
        out_specs=pl.BlockSpec((tm, tn), lambda i, j: (i, j)),
        out_shape=jax.ShapeDtypeStruct((T, N), F32),
        scratch_shapes=[pltpu.VMEM((tm, D), BF16)],
        compiler_params=_params(("arbitrary", "arbitrary")),
        name="in_proj",
    )(x2, w)


def _hgrn2_consts():
    C = CHUNK
    r = np.arange(C)[:, None]
    u = np.arange(C)[None, :]
    mats, masks = [], []
    h = C // 2
    while h >= 1:
        pos = r % (2 * h)
        base = r - pos
        second = pos >= h
        m_q = (u >= base + h) & (u <= r)
        m_k = (u > r) & (u <= base + h - 1)
        mats.append(np.where(second, m_q, m_k))
        masks.append(((r % (2 * h)) >= h) & ((u % (2 * h)) < h) & ((r // (2 * h)) == (u // (2 * h))))
        h //= 2
    mats.append(u <= r)
    mats.append(u > r)
    masks.append(r == u)
    seg = np.concatenate(mats, 0).astype(np.float32)
    msk = np.stack(masks).astype(np.float32)
    return jnp.asarray(seg, BF16), jnp.asarray(msk, F32)


def _hgrn2_kernel(lbp_ref, nw_ref, seg_ref, msk_ref, q_ref, f_ref, i_ref, g_ref, o_ref, st_ref, *, layer):
    C = CHUNK
    n_lvl = msk_ref.shape[0] - 1

    @pl.when(pl.program_id(1) == 0)
    def _():
        st_ref[...] = jnp.zeros_like(st_ref)

    lbp = lbp_ref[...]
    e = jnp.exp(lbp - jnp.max(lbp, axis=0, keepdims=True))
    sm = e / jnp.sum(e, axis=0, keepdims=True)
    lb = jnp.zeros_like(sm[0:1])
    for l in range(1, layer + 1):
        lb = lb + sm[l:l + 1]

    seg = seg_ref[...]
    for h in range(HG_HEADS):
        sl = slice(h * HG_DK, (h + 1) * HG_DK)
        lbh = lb[:, sl]
        fg = lbh + (1.0 - lbh) * _sigmoid(f_ref[:, sl])
        glog = jnp.log(jnp.maximum(fg, GATE_FLOOR))
        kk = 1.0 - fg
        qf = _silu(q_ref[:, sl])
        v = i_ref[:, sl]
        vb = v.astype(BF16)

        decay = jnp.exp(_dot_exact_rhs(seg, glog))
        attn = msk_ref[n_lvl] * _dot_nt(qf.astype(BF16), kk.astype(BF16))
        for l in range(n_lvl):
            dl = decay[l * C:(l + 1) * C]
            attn = attn + msk_ref[l] * _dot_nt((qf * dl).astype(BF16), (kk * dl).astype(BF16))
        e_b = decay[n_lvl * C:(n_lvl + 1) * C]
        e_r = decay[(n_lvl + 1) * C:(n_lvl + 2) * C]

        st = st_ref[h]
        o = _dot(attn.astype(BF16), vb) + _dot_nt((qf * e_b).astype(BF16), st.astype(BF16))
        st_ref[h] = e_b[C - 1:C, :] * st + _dot(v.T.astype(BF16), (kk * e_r).astype(BF16))

        ms = jnp.mean(o * o, axis=-1, keepdims=True)
        o = o * lax.rsqrt(ms + NORM_EPS) * nw_ref[:, sl] * _silu(g_ref[:, sl])
        o_ref[:, sl] = o.astype(o_ref.dtype)


def _hgrn2(proj, lbp, norm_w, layer, B, S):
    seg, msk = _hgrn2_consts()
    nc = S // CHUNK
    col = lambda k: pl.BlockSpec((CHUNK, HG_W), lambda b, c, k=k: (b * nc + c, k))
    const2 = lambda a: pl.BlockSpec(a.shape, lambda b, c: (0,) * a.ndim)
    return pl.pallas_call(
        functools.partial(_hgrn2_kernel, layer=layer),
        grid=(B, nc),
        in_specs=[const2(lbp), const2(norm_w), const2(seg), const2(msk), col(0), col(1), col(2), col(3)],
        out_specs=pl.BlockSpec((CHUNK, HG_W), lambda b, c: (b * nc + c, 0)),
        out_shape=jax.ShapeDtypeStruct((B * S, HG_W), BF16),
        scratch_shapes=[pltpu.VMEM((HG_HEADS, HG_DK, HG_DK), F32)],
        compiler_params=_params(("arbitrary", "arbitrary")),
        name="hgrn2",
    )(lbp, norm_w, seg, msk, proj, proj, proj, proj)


def _ssd_consts():
    C = CHUNK
    r = np.arange(C)[:, None]
    u = np.arange(C)[None, :]
    ltri = (u <= r).astype(np.float32)
    expand = np.zeros((LANES_V7X, SSM_W), np.float32)
    for h in range(SSM_HEADS):
        expand[h, h * SSM_HEADDIM:(h + 1) * SSM_HEADDIM] = 1.0
    return jnp.asarray(ltri, BF16), jnp.asarray(expand, BF16)


def _ssd_kernel(cw_ref, cb_ref, dtb_ref, alog_ref, dsk_ref, nw_ref, ltri_ref, exp_ref,
                z_ref, xbc_ref, dt_ref, o_ref, xpad_ref, st_ref):
    C = CHUNK
    G, N, P = SSM_GROUPS, SSM_STATE, SSM_HEADDIM
    HPG = SSM_HEADS // G
    GW = SSM_W // G
    TAIL = 8

    @pl.when(pl.program_id(1) == 0)
    def _():
        st_ref[...] = jnp.zeros_like(st_ref)
        xpad_ref[0:TAIL, :] = jnp.zeros((TAIL, SSM_CONV_CH), F32)

    xbc = xbc_ref[...]
    xpad_ref[TAIL:TAIL + C, :] = xbc
    u = cb_ref[...] + cw_ref[SSM_CONV - 1:SSM_CONV, :] * xbc
    for j in range(SSM_CONV - 1):
        off = TAIL - (SSM_CONV - 1) + j
        u = u + cw_ref[j:j + 1, :] * xpad_ref[off:off + C, :]
    xpad_ref[0:TAIL, :] = xbc[C - TAIL:C, :]
    u = _silu(u)
    xs = u[:, :SSM_W]

    ltri = ltri_ref[...]
    expand = exp_ref[...]
    a_neg = -jnp.exp(alog_ref[...])
    dt = _softplus(dt_ref[...] + dtb_ref[...])
    acum = _dot_exact_rhs(ltri, dt * a_neg)
    acum_t = acum.T
    dtx = _dot_exact_lhs(dt, expand)
    acx = _dot_exact_lhs(acum, expand)
    acx_last = acx[C - 1:C, :]
    xdt = xs * dtx
    xdt_b = xdt.astype(BF16)
    x_state = (xdt * jnp.exp(acx_last - acx)).astype(BF16)
    e_acx = jnp.exp(acx)

    row = lax.broadcasted_iota(jnp.int32, (C, C), 0)
    colm = lax.broadcasted_iota(jnp.int32, (C, C), 1)
    causal = row >= colm
    lane = lax.broadcasted_iota(jnp.int32, (C, 2 * P), 1)
    first = lane < P

    for g in range(G):
        gs = slice(g * GW, (g + 1) * GW)
        bm = u[:, SSM_W + g * N:SSM_W + (g + 1) * N]
        cm = u[:, SSM_W + G * N + g * N:SSM_W + G * N + (g + 1) * N]
        cmb = cm.astype(BF16)
        cb = _dot_nt(cmb, bm.astype(BF16))
        st = st_ref[g]
        y_off = _dot(cmb, st.astype(BF16))
        diag = []
        for hp in range(HPG // 2):
            pair = g * GW + hp * 2 * P
            xp = xdt_b[:, pair:pair + 2 * P]
            y_pair = None
            for k in range(2):
                hh = g * HPG + hp * 2 + k
                seg = acum[:, hh:hh + 1] - acum_t[hh:hh + 1, :]
                wgt = (cb * jnp.exp(jnp.where(causal, seg, MASK_NEG))).astype(BF16)
                xk = jnp.where(first if k == 0 else jnp.logical_not(first), xp, jnp.zeros_like(xp))
                t = _dot(wgt, xk)
                y_pair = t if y_pair is None else y_pair + t
            diag.append(y_pair)
        st_ref[g] = e_acx[C - 1:C, gs] * st + _dot(bm.T.astype(BF16), x_state[:, gs])

        y = jnp.concatenate(diag, axis=1) + e_acx[:, gs] * y_off + dsk_ref[:, gs] * xs[:, gs]
        y = y * _silu(z_ref[:, gs])
        ms = jnp.mean(y * y, axis=-1, keepdims=True)
        o_ref[:, gs] = (y * lax.rsqrt(ms + NORM_EPS) * nw_ref[:, gs]).astype(o_ref.dtype)


def _ssd(proj, conv_w, conv_b, dt_bias, a_log, d_skip, norm_w, B, S):
    ltri, expand = _ssd_consts()
    nc = S // CHUNK
    const2 = lambda a: pl.BlockSpec(a.shape, lambda b, c: (0,) * a.ndim)
    pad = lambda a: jnp.pad(a, ((0, 0), (0, LANES_V7X - a.shape[1])))
    dtb = pad(dt_bias[None, :])
    alog = pad(a_log[None, :])
    dsk = jnp.repeat(d_skip, SSM_HEADDIM)[None, :]
    args = (conv_w, conv_b[None, :], dtb, alog, dsk, norm_w[None, :], ltri, expand)
    return pl.pallas_call(
        _ssd_kernel,
        grid=(B, nc),
        in_specs=[const2(a) for a in args] + [
            pl.BlockSpec((CHUNK, SSM_W), lambda b, c: (b * nc + c, P_Z // SSM_W)),
            pl.BlockSpec((CHUNK, SSM_CONV_CH), lambda b, c: (b * nc + c, P_XBC // SSM_CONV_CH)),
            pl.BlockSpec((CHUNK, LANES_V7X), lambda b, c: (b * nc + c, P_DT // LANES_V7X))],
        out_specs=pl.BlockSpec((CHUNK, SSM_W), lambda b, c: (b * nc + c, 0)),
        out_shape=jax.ShapeDtypeStruct((B * S, SSM_W), BF16),
        scratch_shapes=[pltpu.VMEM((8 + CHUNK, SSM_CONV_CH), F32),
                        pltpu.VMEM((SSM_GROUPS, SSM_STATE, SSM_W // SSM_GROUPS), F32)],
        compiler_params=_params(("arbitrary", "arbitrary")),
        name="ssd",
    )(*args, proj, proj, proj)


def _moba_kernel(q_ref, k_ref, v_ref, o_ref, kb_ref, vt_ref, km_ref, pen_ref, *, slope):
    BS, Dh = MOBA_BLOCK, ATT_HEAD_DIM
    NB = k_ref.shape[0] // BS
    qb = pl.program_id(2)

    @pl.when(qb == 0)
    def _():
        for j in range(NB):
            kj = k_ref[j * BS:(j + 1) * BS, :]
            kb_ref[j] = kj.astype(BF16)
            vt_ref[j] = v_ref[j * BS:(j + 1) * BS, :].T.astype(BF16)
            km_ref[j:j + 1, :] = jnp.mean(kj, axis=0, keepdims=True)

    q = q_ref[...]
    qs = (q * (Dh ** -0.5)).astype(BF16)

    km_hi, km_lo = _split2(km_ref[...])
    q_hi, q_lo = _split2(q)
    gate = _dot_nt(km_hi, q_hi) + _dot_nt(km_hi, q_lo) + _dot_nt(km_lo, q_hi)
    blk = lax.broadcasted_iota(jnp.int32, (NB, BS), 0)
    gate = jnp.where(blk < qb, gate, MASK_NEG)
    pen = jnp.full((NB, BS), MASK_NEG, F32)
    offset = (-slope * BS) * (qb - blk).astype(F32)
    for r in range(MOBA_TOPK):
        mx = jnp.max(gate, axis=0, keepdims=True)
        idx = jnp.min(jnp.where(gate == mx, blk, NB), axis=0, keepdims=True)
        pick = blk == idx
        pen = jnp.where(jnp.logical_and(pick, (jnp.zeros_like(blk) + r) < qb), offset, pen)
        gate = jnp.where(pick, -jnp.inf, gate)
    pen_ref[...] = pen

    key_i = lax.broadcasted_iota(jnp.int32, (BS, BS), 0)
    qry_i = lax.broadcasted_iota(jnp.int32, (BS, BS), 1)
    alibi = slope * key_i.astype(F32)

    s = _dot_nt(kb_ref[qb], qs) + alibi
    s = jnp.where(key_i <= qry_i, s, MASK_NEG)
    m = jnp.max(s, axis=0, keepdims=True)
    p = jnp.exp(s - m)
    l = jnp.sum(p, axis=0, keepdims=True)
    acc = _dot(vt_ref[qb], p.astype(BF16))

    def body(j, carry):
        m, l, acc = carry
        s = _dot_nt(kb_ref[j], qs) + alibi + pen_ref[pl.ds(j, 1), :]
        m_new = jnp.maximum(m, jnp.max(s, axis=0, keepdims=True))
        alpha = jnp.exp(m - m_new)
        p = jnp.exp(s - m_new)
        l = alpha * l + jnp.sum(p, axis=0, keepdims=True)
        acc = alpha * acc + _dot(vt_ref[j], p.astype(BF16))
        return m_new, l, acc

    m, l, acc = lax.fori_loop(0, qb, body, (m, l, acc))
    o_ref[...] = (acc / l).T.astype(o_ref.dtype)


def _moba(proj, B, S):
    BS, Dh = MOBA_BLOCK, ATT_HEAD_DIM
    nb = S // BS
    outs = []
    q0 = P_ATT // Dh
    for h in range(ATT_HEADS):
        slope = float(2.0 ** (-8.0 * (h + 1) / ATT_HEADS))
        outs.append(pl.pallas_call(
            functools.partial(_moba_kernel, slope=slope),
            grid=(B, 1, nb),
            in_specs=[pl.BlockSpec((BS, Dh), lambda b, _, i, h=h: (b * nb + i, q0 + h)),
                      pl.BlockSpec((S, Dh), lambda b, _, i, h=h: (b, q0 + ATT_HEADS + h)),
                      pl.BlockSpec((S, Dh), lambda b, _, i, h=h: (b, q0 + 2 * ATT_HEADS + h))],
            out_specs=pl.BlockSpec((BS, Dh), lambda b, _, i: (b * nb + i, 0)),
            out_shape=jax.ShapeDtypeStruct((B * S, Dh), BF16),
            scratch_shapes=[pltpu.VMEM((nb, BS, Dh), BF16), pltpu.VMEM((nb, Dh, BS), BF16),
                            pltpu.VMEM((nb, Dh), F32), pltpu.VMEM((nb, BS), F32)],
            compiler_params=_params(("arbitrary", "arbitrary", "arbitrary")),
            name=f"moba_h{h}",
        )(proj, proj, proj))
    return outs


def _layer_norm(r, g, b):
    mu = jnp.mean(r, axis=-1, keepdims=True)
    d = r - mu
    var = jnp.mean(d * d, axis=-1, keepdims=True)
    return d * lax.rsqrt(var + NORM_EPS) * g + b


def _outproj_kernel(*refs, n_parts):
    y_refs = refs[:n_parts]
    w_refs = refs[n_parts:2 * n_parts]
    x_ref, g_ref, b_ref, o_ref = refs[2 * n_parts:]
    mix = _dot(y_refs[0][...], w_refs[0][...])
    for y_ref, w_ref in zip(y_refs[1:], w_refs[1:]):
        mix = mix + _dot(y_ref[...], w_ref[...])
    o_ref[...] = _layer_norm(DEEPNORM_ALPHA * x_ref[...] + mix, g_ref[...], b_ref[...])


def _out_proj(ys, ws, x2, g, b, tm):
    T, D = x2.shape
    n = len(ys)
    row = lambda a: pl.BlockSpec((tm, a.shape[1]), lambda i: (i, 0))
    whole = lambda a: pl.BlockSpec(a.shape, lambda i: (0, 0))
    return pl.pallas_call(
        functools.partial(_outproj_kernel, n_parts=n),
        grid=(T // tm,),
        in_specs=[row(y) for y in ys] + [whole(w) for w in ws] + [row(x2), whole(g), whole(b)],
        out_specs=pl.BlockSpec((tm, D), lambda i: (i, 0)),
        out_shape=jax.ShapeDtypeStruct((T, D), F32),
        compiler_params=_params(("arbitrary",)),
        name="out_proj_ln",
    )(*ys, *ws, x2, g, b)


def _mlp_kernel(h_ref, w1_ref, w2_ref, g_ref, b_ref, o_ref, hb_ref, acc_ref):
    j = pl.program_id(1)

    @pl.when(j == 0)
    def _():
        hb_ref[...] = h_ref[...].astype(BF16)
        acc_ref[...] = jnp.zeros_like(acc_ref)

    a = jnp.maximum(_dot(hb_ref[...], w1_ref[...]), 0.0)
    acc_ref[...] += _dot((a * a).astype(BF16), w2_ref[...])

    @pl.when(j == pl.num_programs(1) - 1)
    def _():
        o_ref[...] = _layer_norm(DEEPNORM_ALPHA * h_ref[...] + acc_ref[...], g_ref[...], b_ref[...])


def _mlp(h2, w1, w2, g, b, tm, tf):
    T, D = h2.shape
    F = w1.shape[1]
    return pl.pallas_call(
        _mlp_kernel,
        grid=(T // tm, F // tf),
        in_specs=[pl.BlockSpec((tm, D), lambda i, j: (i, 0)),
                  pl.BlockSpec((D, tf), lambda i, j: (0, j)),
                  pl.BlockSpec((tf, D), lambda i, j: (j, 0)),
                  pl.BlockSpec((1, D), lambda i, j: (0, 0)),
                  pl.BlockSpec((1, D), lambda i, j: (0, 0))],
        out_specs=pl.BlockSpec((tm, D), lambda i, j: (i, 0)),
        out_shape=jax.ShapeDtypeStruct((T, D), F32),
        scratch_shapes=[pltpu.VMEM((tm, D), BF16), pltpu.VMEM((tm, D), F32)],
        compiler_params=_params(("arbitrary", "arbitrary")),
        name="mlp_ln",
    )(h2, w1, w2, g, b)


def _tiles(T):
    pick = lambda pref: next(t for t in (pref, 512, 256, 128) if T % t == 0)
    return dict(tm_in=pick(1024), tn_in=896, tm_out=pick(512), tm_mlp=pick(512), tf_mlp=512)


def kernel(x, w_in, hg_lower_bounds, hg_norm_w, ssm_conv_w, ssm_conv_b, ssm_dt_bias, ssm_a_log, ssm_d, ssm_norm_w,
           w_out, ln1_g, ln1_b, w_mlp_in, w_mlp_out, ln2_g, ln2_b):
    B, S, D = x.shape
    assert D == D_MODEL and S % MOBA_BLOCK == 0 and S % CHUNK == 0
    T = B * S
    tl = _tiles(T)
    x2 = x.reshape(T, D).astype(F32)
    lbp = hg_lower_bounds.astype(F32)
    for l in range(DEPTH):
        wl = w_in[l]
        w_r = jnp.concatenate([wl[:, :SSM_DT_OFF], wl[:, ATT_OFF:], wl[:, SSM_DT_OFF:ATT_OFF],
                               jnp.zeros((D, LANES_V7X - SSM_HEADS), wl.dtype)], axis=1).astype(BF16)
        proj = _in_proj(x2, w_r, tl["tm_in"], tl["tn_in"])
        y_hg = _hgrn2(proj, lbp, hg_norm_w[l][None, :].astype(F32), l, B, S)
        y_ssm = _ssd(proj, ssm_conv_w[l].astype(F32), ssm_conv_b[l].astype(F32), ssm_dt_bias[l].astype(F32),
                     ssm_a_log[l].astype(F32), ssm_d[l].astype(F32), ssm_norm_w[l].astype(F32), B, S)
        y_att = _moba(proj, B, S)
        wo = w_out[l].astype(BF16)
        ws = [wo[:HG_W], wo[HG_W:HG_W + SSM_W]] + [
            wo[HG_W + SSM_W + h * ATT_HEAD_DIM:HG_W + SSM_W + (h + 1) * ATT_HEAD_DIM] for h in range(ATT_HEADS)]
        h2 = _out_proj([y_hg, y_ssm] + y_att, ws, x2, ln1_g[l][None, :].astype(F32), ln1_b[l][None, :].astype(F32),
                       tl["tm_out"])
        x2 = _mlp(h2, w_mlp_in[l].astype(BF16), w_mlp_out[l].astype(BF16), ln2_g[l][None, :].astype(F32),
                  ln2_b[l][None, :].astype(F32), tl["tm_mlp"], tl["tf_mlp"])
    return x2.reshape(B, S, D).astype(x.dtype)
```

```python
import functools

import numpy as np
import jax
import jax.numpy as jnp
from jax import lax
from jax.experimental import pallas as pl
from jax.experimental.pallas import tpu as pltpu

F32 = jnp.float32
BF16 = jnp.bfloat16

LANES_V7X = 128
SUBLANES_V7X = 8
VMEM_LIMIT_BYTES_V7X = 60000 * 1024

D_MODEL = 2048
DEPTH = 2
HG_W = D_MODEL // 4
SSM_W = D_MODEL // 2
ATT_W = D_MODEL - HG_W - SSM_W
HG_DK = 128
HG_HEADS = HG_W // HG_DK
SSM_HEADDIM = 64
SSM_HEADS = SSM_W // SSM_HEADDIM
SSM_GROUPS = 2
SSM_STATE = 128
SSM_CONV = 4
SSM_CONV_CH = SSM_W + 2 * SSM_GROUPS * SSM_STATE
ATT_HEAD_DIM = 128
ATT_HEADS = ATT_W // ATT_HEAD_DIM
MOBA_BLOCK = 256
MOBA_TOPK = 3
D_FF = 4 * D_MODEL
DEEPNORM_ALPHA = (2 * DEPTH) ** 0.25
NORM_EPS = 1e-5
MASK_NEG = -1e30
GATE_FLOOR = 1e-30

SSM_Z_OFF = 4 * HG_W
SSM_XBC_OFF = SSM_Z_OFF + SSM_W
SSM_DT_OFF = SSM_XBC_OFF + SSM_CONV_CH
ATT_OFF = SSM_DT_OFF + SSM_HEADS
N_IN = ATT_OFF + 3 * ATT_W

P_Z = 4 * HG_W
P_XBC = P_Z + SSM_W
P_ATT = P_XBC + SSM_CONV_CH
P_DT = P_ATT + 3 * ATT_W
N_PROJ = P_DT + LANES_V7X

CHUNK = 128

_NT = (((1,), (1,)), ((), ()))


def _dot(a, b):
    return jnp.dot(a, b, preferred_element_type=F32)


def _dot_nt(a, b):
    return lax.dot_general(a, b, _NT, preferred_element_type=F32)


def _split2(x):
    hi = x.astype(BF16)
    lo = (x - hi.astype(F32)).astype(BF16)
    return hi, lo


def _split3(x):
    hi = x.astype(BF16)
    r = x - hi.astype(F32)
    mid = r.astype(BF16)
    lo = (r - mid.astype(F32)).astype(BF16)
    return hi, mid, lo


def _dot_exact_rhs(m, x):
    hi, lo = _split2(x)
    return _dot(m, hi) + _dot(m, lo)


def _dot_exact_lhs(x, m):
    hi, lo = _split2(x)
    return _dot(hi, m) + _dot(lo, m)


def _sigmoid(x):
    return 1.0 / (1.0 + jnp.exp(-x))


def _silu(x):
    return x * _sigmoid(x)


def _softplus(x):
    return jnp.maximum(x, 0.0) + jnp.log(1.0 + jnp.exp(-jnp.abs(x)))


def _params(sem):
    return pltpu.CompilerParams(dimension_semantics=sem, vmem_limit_bytes=VMEM_LIMIT_BYTES_V7X)


def _inproj_kernel(x_ref, w_ref, o_ref, xb_ref):
    @pl.when(pl.program_id(1) == 0)
    def _():
        xb_ref[...] = x_ref[...].astype(BF16)

    o_ref[...] = _dot(xb_ref[...], w_ref[...])


def _in_proj(x2, w, tm, tn):
    T, D = x2.shape
    N = w.shape[1]
    return pl.pallas_call(
        _inproj_kernel,
        grid=(T // tm, N // tn),
        in_specs=[pl.BlockSpec((tm, D), lambda i, j: (i, 0)),
                  pl.BlockSpec((D, tn), lambda i, j: (0, j))],
        out_specs=pl.BlockSpec((tm, tn), lambda i, j: (i, j)),
        out_shape=jax.ShapeDtypeStruct((T, N), F32),
        scratch_shapes=[pltpu.VMEM((tm, D), BF16)],
        compiler_params=_params(("arbitrary", "arbitrary")),
        name="in_proj",
    )(x2, w)


HG_ROW_LEVELS = tuple(h for h in (64, 32, 16, 8, 4) if h < CHUNK)
HG_DOT_LEVELS = (2, 1)


def _hgrn2_consts():
    C = CHUNK
    r = np.arange(C)[:, None]
    u = np.arange(C)[None, :]
    small, masks = [], []
    for h in HG_ROW_LEVELS + HG_DOT_LEVELS:
        masks.append(((r % (2 * h)) >= h) & ((u % (2 * h)) < h) & ((r // (2 * h)) == (u // (2 * h))))
    for h in HG_DOT_LEVELS:
        pos = r % (2 * h)
        base = r - pos
        m_q = (u >= base + h) & (u <= r)
        m_k = (u > r) & (u <= base + h - 1)
        small.append(np.where(pos >= h, m_q, m_k))
    masks.append(r == u)
    ltri = (u <= r).astype(np.float32)
    seg = np.concatenate(small, 0).astype(np.float32)
    msk = np.stack(masks).astype(np.float32)
    return jnp.asarray(ltri, BF16), jnp.asarray(seg, BF16), jnp.asarray(msk, F32)


def _hgrn2_kernel(lbp_ref, nw_ref, ltri_ref, seg_ref, msk_ref, q_ref, f_ref, i_ref, g_ref, o_ref,
                  st_ref, b_ref, *, layer):
    C, dk = CHUNK, HG_DK
    n_row, n_dot = len(HG_ROW_LEVELS), len(HG_DOT_LEVELS)

    @pl.when(pl.program_id(1) == 0)
    def _():
        st_ref[...] = jnp.zeros_like(st_ref)

    lbp = lbp_ref[...]
    e = jnp.exp(lbp - jnp.max(lbp, axis=0, keepdims=True))
    sm = e / jnp.sum(e, axis=0, keepdims=True)
    lb = jnp.zeros_like(sm[0:1])
    for l in range(1, layer + 1):
        lb = lb + sm[l:l + 1]

    fg = lb + (1.0 - lb) * _sigmoid(f_ref[...])
    glog = jnp.log2(jnp.maximum(fg, GATE_FLOOR))
    kk_all = 1.0 - fg
    qf_all = _silu(q_ref[...])
    g_hi, g_mid, g_lo = _split3(glog)
    ltri = ltri_ref[...]
    b_all = _dot(ltri, g_hi) + _dot(ltri, g_mid) + _dot(ltri, g_lo)
    b_ref[...] = b_all
    d_small = _dot_exact_rhs(seg_ref[...], glog)

    row8 = lax.broadcasted_iota(jnp.int32, (SUBLANES_V7X, dk), 0)
    sgn4 = jnp.where(row8 >= 4, 1.0, -1.0)

    for h in range(HG_HEADS):
        sl = slice(h * dk, (h + 1) * dk)
        b = b_all[:, sl]
        kk = kk_all[:, sl]
        qf = qf_all[:, sl]
        v = i_ref[:, sl]
        vb = v.astype(BF16)

        attn = msk_ref[n_row + n_dot] * _dot_nt(qf.astype(BF16), kk.astype(BF16))
        for l, hb in enumerate(HG_ROW_LEVELS):
            pieces = []
            for i in range(C // (2 * hb)):
                lo = 2 * hb * i
                row = b_ref[lo + hb - 1:lo + hb, sl]
                if hb >= SUBLANES_V7X:
                    bm = jnp.broadcast_to(row, (hb, dk))
                    pieces.append(bm - b[lo:lo + hb])
                    pieces.append(b[lo + hb:lo + 2 * hb] - bm)
                else:
                    pieces.append((b[lo:lo + 2 * hb] - jnp.broadcast_to(row, (2 * hb, dk))) * sgn4)
            dl = jnp.exp2(jnp.concatenate(pieces, axis=0))
            attn = attn + msk_ref[l] * _dot_nt((qf * dl).astype(BF16), (kk * dl).astype(BF16))
        for l in range(n_dot):
            dl = jnp.exp2(d_small[l * C:(l + 1) * C, sl])
            attn = attn + msk_ref[n_row + l] * _dot_nt((qf * dl).astype(BF16), (kk * dl).astype(BF16))

        b_last = b_ref[C - 1:C, sl]
        e_b = jnp.exp2(b)
        e_r = jnp.exp2(b_last - b)
        st = st_ref[h]
        o = _dot(attn.astype(BF16), vb) + _dot_nt((qf * e_b).astype(BF16), st.astype(BF16))
        st_ref[h] = jnp.exp2(b_last) * st + _dot(v.T.astype(BF16), (kk * e_r).astype(BF16))

        ms = jnp.mean(o * o, axis=-1, keepdims=True)
        o = o * lax.rsqrt(ms + NORM_EPS) * nw_ref[:, sl] * _silu(g_ref[:, sl])
        o_ref[:, sl] = o.astype(o_ref.dtype)


def _hgrn2(proj, lbp, norm_w, layer, B, S):
    ltri, seg, msk = _hgrn2_consts()
    nc = S // CHUNK
    col = lambda k: pl.BlockSpec((CHUNK, HG_W), lambda b, c, k=k: (b * nc + c, k))
    const2 = lambda a: pl.BlockSpec(a.shape, lambda b, c: (0,) * a.ndim)
    return pl.pallas_call(
        functools.partial(_hgrn2_kernel, layer=layer),
        grid=(B, nc),
        in_specs=[const2(lbp), const2(norm_w), const2(ltri), const2(seg), const2(msk),
                  col(0), col(1), col(2), col(3)],
        out_specs=pl.BlockSpec((CHUNK, HG_W), lambda b, c: (b * nc + c, 0)),
        out_shape=jax.ShapeDtypeStruct((B * S, HG_W), BF16),
        scratch_shapes=[pltpu.VMEM((HG_HEADS, HG_DK, HG_DK), F32), pltpu.VMEM((CHUNK, HG_W), F32)],
        compiler_params=_params(("arbitrary", "arbitrary")),
        name="hgrn2",
    )(lbp, norm_w, ltri, seg, msk, proj, proj, proj, proj)


def _ssd_consts():
    C = CHUNK
    r = np.arange(C)[:, None]
    u = np.arange(C)[None, :]
    ltri = (u <= r).astype(np.float32)
    expand = np.zeros((LANES_V7X, SSM_W), np.float32)
    for h in range(SSM_HEADS):
        expand[h, h * SSM_HEADDIM:(h + 1) * SSM_HEADDIM] = 1.0
    return jnp.asarray(ltri, BF16), jnp.asarray(expand, BF16)


def _ssd_kernel(cw_ref, cb_ref, dtb_ref, alog_ref, dsk_ref, nw_ref, ltri_ref, exp_ref,
                z_ref, xbc_ref, dt_ref, o_ref, xpad_ref, st_ref):
    C = CHUNK
    G, N, P = SSM_GROUPS, SSM_STATE, SSM_HEADDIM
    HPG = SSM_HEADS // G
    GW = SSM_W // G
    TAIL = SUBLANES_V7X

    @pl.when(pl.program_id(1) == 0)
    def _():
        st_ref[...] = jnp.zeros_like(st_ref)
        xpad_ref[0:TAIL, :] = jnp.zeros((TAIL, SSM_CONV_CH), F32)

    xbc = xbc_ref[...]
    xpad_ref[TAIL:TAIL + C, :] = xbc
    u = cb_ref[...] + cw_ref[SSM_CONV - 1:SSM_CONV, :] * xbc
    for j in range(SSM_CONV - 1):
        off = TAIL - (SSM_CONV - 1) + j
        u = u + cw_ref[j:j + 1, :] * xpad_ref[off:off + C, :]
    xpad_ref[0:TAIL, :] = xbc[C - TAIL:C, :]
    u = _silu(u)
    xs = u[:, :SSM_W]

    ltri = ltri_ref[...]
    expand = exp_ref[...]
    a_neg = -jnp.exp(alog_ref[...])
    dt = _softplus(dt_ref[...] + dtb_ref[...])
    acum = _dot_exact_rhs(ltri, dt * a_neg)
    acum_t = acum.T
    dtx = _dot_exact_lhs(dt, expand)
    acx = _dot_exact_lhs(acum, expand)
    acx_last = acx[C - 1:C, :]
    xdt = xs * dtx
    xdt_b = xdt.astype(BF16)
    x_state = (xdt * jnp.exp(acx_last - acx)).astype(BF16)
    e_acx = jnp.exp(acx)

    row = lax.broadcasted_iota(jnp.int32, (C, C), 0)
    colm = lax.broadcasted_iota(jnp.int32, (C, C), 1)
    causal = row >= colm
    lane = lax.broadcasted_iota(jnp.int32, (C, 2 * P), 1)
    first = lane < P

    for g in range(G):
        gs = slice(g * GW, (g + 1) * GW)
        bm = u[:, SSM_W + g * N:SSM_W + (g + 1) * N]
        cm = u[:, SSM_W + G * N + g * N:SSM_W + G * N + (g + 1) * N]
        cmb = cm.astype(BF16)
        cb = _dot_nt(cmb, bm.astype(BF16))
        st = st_ref[g]
        y_off = _dot(cmb, st.astype(BF16))
        diag = []
        for hp in range(HPG // 2):
            pair = g * GW + hp * 2 * P
            xp = xdt_b[:, pair:pair + 2 * P]
            y_pair = None
            for k in range(2):
                hh = g * HPG + hp * 2 + k
                seg = acum[:, hh:hh + 1] - acum_t[hh:hh + 1, :]
                wgt = (cb * jnp.exp(jnp.where(causal, seg, MASK_NEG))).astype(BF16)
                xk = jnp.where(first if k == 0 else jnp.logical_not(first), xp, jnp.zeros_like(xp))
                t = _dot(wgt, xk)
                y_pair = t if y_pair is None else y_pair + t
            diag.append(y_pair)
        st_ref[g] = e_acx[C - 1:C, gs] * st + _dot(bm.T.astype(BF16), x_state[:, gs])

        y = jnp.concatenate(diag, axis=1) + e_acx[:, gs] * y_off + dsk_ref[:, gs] * xs[:, gs]
        y = y * _silu(z_ref[:, gs])
        ms = jnp.mean(y * y, axis=-1, keepdims=True)
        o_ref[:, gs] = (y * lax.rsqrt(ms + NORM_EPS) * nw_ref[:, gs]).astype(o_ref.dtype)


def _ssd(proj, conv_w, conv_b, dt_bias, a_log, d_skip, norm_w, B, S):
    ltri, expand = _ssd_consts()
    nc = S // CHUNK
    const2 = lambda a: pl.BlockSpec(a.shape, lambda b, c: (0,) * a.ndim)
    pad = lambda a: jnp.pad(a, ((0, 0), (0, LANES_V7X - a.shape[1])))
    dtb = pad(dt_bias[None, :])
    alog = pad(a_log[None, :])
    dsk = jnp.repeat(d_skip, SSM_HEADDIM)[None, :]
    args = (conv_w, conv_b[None, :], dtb, alog, dsk, norm_w[None, :], ltri, expand)
    return pl.pallas_call(
        _ssd_kernel,
        grid=(B, nc),
        in_specs=[const2(a) for a in args] + [
            pl.BlockSpec((CHUNK, SSM_W), lambda b, c: (b * nc + c, P_Z // SSM_W)),
            pl.BlockSpec((CHUNK, SSM_CONV_CH), lambda b, c: (b * nc + c, P_XBC // SSM_CONV_CH)),
            pl.BlockSpec((CHUNK, LANES_V7X), lambda b, c: (b * nc + c, P_DT // LANES_V7X))],
        out_specs=pl.BlockSpec((CHUNK, SSM_W), lambda b, c: (b * nc + c, 0)),
        out_shape=jax.ShapeDtypeStruct((B * S, SSM_W), BF16),
        scratch_shapes=[pltpu.VMEM((SUBLANES_V7X + CHUNK, SSM_CONV_CH), F32),
                        pltpu.VMEM((SSM_GROUPS, SSM_STATE, SSM_W // SSM_GROUPS), F32)],
        compiler_params=_params(("arbitrary", "arbitrary")),
        name="ssd",
    )(*args, proj, proj, proj)


MOBA_PEN0 = SUBLANES_V7X


def _moba_prep_kernel(slope_ref, k_ref, v_ref, ka_ref, vt_ref, km_ref):
    BS, Dh = MOBA_BLOCK, ATT_HEAD_DIM
    NB = k_ref.shape[0] // BS
    slope = slope_ref[0:1, 0:Dh]
    key_i = lax.broadcasted_iota(jnp.int32, (BS, Dh), 0).astype(F32)
    lane = lax.broadcasted_iota(jnp.int32, (BS, Dh), 1)
    for j in range(NB):
        kj = k_ref[j * BS:(j + 1) * BS, :]
        extra = jnp.where(lane == 0, slope * key_i, jnp.where(lane == MOBA_PEN0 + j, 1.0, 0.0))
        ka_ref[j] = jnp.concatenate([kj.astype(BF16), extra.astype(BF16)], axis=1)
        vt_ref[j] = v_ref[j * BS:(j + 1) * BS, :].T.astype(BF16)
        km_ref[j:j + 1, :] = jnp.mean(kj, axis=0, keepdims=True)


def _moba_kernel(slope_ref, qa_ref, qb_ref, ka_ref, vt_ref, km_ref, oa_ref, ob_ref, qaug_ref, s_ref):
    BS, Dh = MOBA_BLOCK, ATT_HEAD_DIM
    NB = ka_ref.shape[0]
    p = pl.program_id(2)
    slope = slope_ref[0:1, :]
    blk = lax.broadcasted_iota(jnp.int32, (NB, BS), 0)
    km_hi, km_lo = _split2(km_ref[...])

    def augmented_queries(q_ref, qblk, slot):
        q = q_ref[...]
        q_hi, q_lo = _split2(q)
        gate = _dot_nt(km_hi, q_hi) + _dot_nt(km_hi, q_lo) + _dot_nt(km_lo, q_hi)
        gate = jnp.where(blk < qblk, gate, MASK_NEG)
        pen = jnp.full((NB, BS), MASK_NEG, F32)
        offset = (-BS) * slope * (qblk - blk).astype(F32)
        for r in range(min(MOBA_TOPK, NB)):
            mx = jnp.max(gate, axis=0, keepdims=True)
            idx = jnp.min(jnp.where(gate == mx, blk, NB), axis=0, keepdims=True)
            pick = blk == idx
            pen = jnp.where(jnp.logical_and(pick, (jnp.zeros_like(blk) + r) < qblk), offset, pen)
            gate = jnp.where(pick, -jnp.inf, gate)
        pen = jnp.where(blk == qblk, 0.0, pen)
        extra_t = jnp.concatenate([jnp.ones((MOBA_PEN0, BS), F32), pen,
                                   jnp.zeros((Dh - MOBA_PEN0 - NB, BS), F32)], axis=0)
        qaug_ref[slot] = jnp.concatenate([(q * (Dh ** -0.5)).astype(BF16), extra_t.T.astype(BF16)], axis=1)

    qblk_a = p
    qblk_b = NB - 1 - p
    augmented_queries(qa_ref, qblk_a, 0)
    augmented_queries(qb_ref, qblk_b, 1)

    key_i = lax.broadcasted_iota(jnp.int32, (BS, BS), 0)
    qry_i = lax.broadcasted_iota(jnp.int32, (BS, BS), 1)
    causal = key_i <= qry_i
    neg_row = jnp.full((1, BS), -jnp.inf, F32)

    s_own_a = jnp.where(causal, _dot_nt(ka_ref[qblk_a], qaug_ref[0]), MASK_NEG)
    s_own_b = jnp.where(causal, _dot_nt(ka_ref[qblk_b], qaug_ref[1]), MASK_NEG)
    m_a = jnp.max(s_own_a, axis=0, keepdims=True)
    m_b = jnp.max(s_own_b, axis=0, keepdims=True)
    tiles = []
    for t in range(NB - 1):
        of_b = jnp.where(t >= p, 1, 0)
        j = t - of_b * p
        in_a = (jnp.zeros((1, BS), jnp.int32) + t) < p
        s = _dot_nt(ka_ref[j], qaug_ref[of_b])
        s_ref[t] = s
        cm = jnp.max(s, axis=0, keepdims=True)
        m_a = jnp.maximum(m_a, jnp.where(in_a, cm, neg_row))
        m_b = jnp.maximum(m_b, jnp.where(in_a, neg_row, cm))
        tiles.append((j, in_a))

    p_a = jnp.exp(s_own_a - m_a)
    p_b = jnp.exp(s_own_b - m_b)
    l_a = jnp.sum(p_a, axis=0, keepdims=True)
    l_b = jnp.sum(p_b, axis=0, keepdims=True)
    acc_a = _dot(vt_ref[qblk_a], p_a.astype(BF16))
    acc_b = _dot(vt_ref[qblk_b], p_b.astype(BF16))
    run = jnp.zeros((Dh, BS), F32)
    run_a = jnp.zeros((Dh, BS), F32)
    for t, (j, in_a) in enumerate(tiles):
        pt = jnp.exp(s_ref[t] - jnp.where(in_a, m_a, m_b))
        ls = jnp.sum(pt, axis=0, keepdims=True)
        l_a = l_a + jnp.where(in_a, ls, 0.0)
        l_b = l_b + jnp.where(in_a, 0.0, ls)
        run = run + _dot(vt_ref[j], pt.astype(BF16))
        run_a = jnp.where((jnp.zeros((Dh, BS), jnp.int32) + (t + 1)) == p, run, run_a)
    oa_ref[...] = ((acc_a + run_a) * (1.0 / l_a)).T.astype(oa_ref.dtype)
    ob_ref[...] = ((acc_b + (run - run_a)) * (1.0 / l_b)).T.astype(ob_ref.dtype)


def _moba(proj, B, S):
    BS, Dh, H = MOBA_BLOCK, ATT_HEAD_DIM, ATT_HEADS
    nb = S // BS
    assert nb % 2 == 0 and nb % SUBLANES_V7X == 0 and MOBA_PEN0 + nb <= Dh
    npair = nb // 2
    q0 = P_ATT // Dh
    slopes = np.asarray([2.0 ** (-8.0 * (h + 1) / H) for h in range(H)], np.float32)
    probe = np.concatenate([slopes[:, None] * np.arange(BS)[None, :], slopes[:, None] * BS * np.arange(nb)[None, :]], 1)
    assert np.array_equal(probe.astype(BF16).astype(np.float32), probe)
    slope_tab = jnp.asarray(np.broadcast_to(slopes[:, None, None], (H, SUBLANES_V7X, BS)))
    slope_spec = lambda nd: pl.BlockSpec((None, SUBLANES_V7X, BS), (lambda b, h: (h, 0, 0)) if nd == 2
                                         else (lambda b, h, p: (h, 0, 0)))

    ka, vt, km = pl.pallas_call(
        _moba_prep_kernel,
        grid=(B, H),
        in_specs=[slope_spec(2),
                  pl.BlockSpec((S, Dh), lambda b, h: (b, q0 + H + h)),
                  pl.BlockSpec((S, Dh), lambda b, h: (b, q0 + 2 * H + h))],
        out_specs=[pl.BlockSpec((None, nb, BS, 2 * Dh), lambda b, h: (b * H + h, 0, 0, 0)),
                   pl.BlockSpec((None, nb, Dh, BS), lambda b, h: (b * H + h, 0, 0, 0)),
                   pl.BlockSpec((None, nb, Dh), lambda b, h: (b * H + h, 0, 0))],
        out_shape=[jax.ShapeDtypeStruct((B * H, nb, BS, 2 * Dh), BF16),
                   jax.ShapeDtypeStruct((B * H, nb, Dh, BS), BF16),
                   jax.ShapeDtypeStruct((B * H, nb, Dh), F32)],
        compiler_params=_params(("arbitrary", "arbitrary")),
        name="moba_prep",
    )(slope_tab, proj, proj)

    out = jax.ShapeDtypeStruct((B * npair * BS, H * Dh), BF16)
    oa, ob = pl.pallas_call(
        _moba_kernel,
        grid=(B, H, npair),
        in_specs=[slope_spec(3),
                  pl.BlockSpec((BS, Dh), lambda b, h, p: (b * nb + p, q0 + h)),
                  pl.BlockSpec((BS, Dh), lambda b, h, p: (b * nb + nb - 1 - p, q0 + h)),
                  pl.BlockSpec((None, nb, BS, 2 * Dh), lambda b, h, p: (b * H + h, 0, 0, 0)),
                  pl.BlockSpec((None, nb, Dh, BS), lambda b, h, p: (b * H + h, 0, 0, 0)),
                  pl.BlockSpec((None, nb, Dh), lambda b, h, p: (b * H + h, 0, 0))],
        out_specs=[pl.BlockSpec((BS, Dh), lambda b, h, p: (b * npair + p, h)),
                   pl.BlockSpec((BS, Dh), lambda b, h, p: (b * npair + p, h))],
        out_shape=[out, out],
        scratch_shapes=[pltpu.VMEM((2, BS, 2 * Dh), BF16), pltpu.VMEM((max(nb - 1, 1), BS, BS), F32)],
        compiler_params=_params(("arbitrary", "arbitrary", "arbitrary")),
        name="moba",
    )(slope_tab, proj, proj, ka, vt, km)
    oa = oa.reshape(B, npair, BS, H * Dh)
    ob = ob.reshape(B, npair, BS, H * Dh)
    return jnp.concatenate([oa, ob[:, ::-1]], axis=1).reshape(B * S, H * Dh)


def _layer_norm(r, g, b):
    mu = jnp.mean(r, axis=-1, keepdims=True)
    d = r - mu
    var = jnp.mean(d * d, axis=-1, keepdims=True)
    return d * lax.rsqrt(var + NORM_EPS) * g + b


def _outproj_kernel(*refs, n_parts):
    y_refs = refs[:n_parts]
    w_refs = refs[n_parts:2 * n_parts]
    x_ref, g_ref, b_ref, o_ref = refs[2 * n_parts:]
    mix = _dot(y_refs[0][...], w_refs[0][...])
    for y_ref, w_ref in zip(y_refs[1:], w_refs[1:]):
        mix = mix + _dot(y_ref[...], w_ref[...])
    o_ref[...] = _layer_norm(DEEPNORM_ALPHA * x_ref[...] + mix, g_ref[...], b_ref[...])


def _out_proj(ys, ws, x2, g, b, tm):
    T, D = x2.shape
    n = len(ys)
    row = lambda a: pl.BlockSpec((tm, a.shape[1]), lambda i: (i, 0))
    whole = lambda a: pl.BlockSpec(a.shape, lambda i: (0, 0))
    return pl.pallas_call(
        functools.partial(_outproj_kernel, n_parts=n),
        grid=(T // tm,),
        in_specs=[row(y) for y in ys] + [whole(w) for w in ws] + [row(x2), whole(g), whole(b)],
        out_specs=pl.BlockSpec((tm, D), lambda i: (i, 0)),
        out_shape=jax.ShapeDtypeStruct((T, D), F32),
        compiler_params=_params(("arbitrary",)),
        name="out_proj_ln",
    )(*ys, *ws, x2, g, b)


def _mlp_kernel(h_ref, w1_ref, w2_ref, g_ref, b_ref, o_ref, hb_ref, acc_ref):
    j = pl.program_id(1)

    @pl.when(j == 0)
    def _():
        hb_ref[...] = h_ref[...].astype(BF16)
        acc_ref[...] = jnp.zeros_like(acc_ref)

    a = jnp.maximum(_dot(hb_ref[...], w1_ref[...]), 0.0)
    acc_ref[...] += _dot((a * a).astype(BF16), w2_ref[...])

    @pl.when(j == pl.num_programs(1) - 1)
    def _():
        o_ref[...] = _layer_norm(DEEPNORM_ALPHA * h_ref[...] + acc_ref[...], g_ref[...], b_ref[...])


def _mlp(h2, w1, w2, g, b, tm, tf):
    T, D = h2.shape
    F = w1.shape[1]
    return pl.pallas_call(
        _mlp_kernel,
        grid=(T // tm, F // tf),
        in_specs=[pl.BlockSpec((tm, D), lambda i, j: (i, 0)),
                  pl.BlockSpec((D, tf), lambda i, j: (0, j)),
                  pl.BlockSpec((tf, D), lambda i, j: (j, 0)),
                  pl.BlockSpec((1, D), lambda i, j: (0, 0)),
                  pl.BlockSpec((1, D), lambda i, j: (0, 0))],
        out_specs=pl.BlockSpec((tm, D), lambda i, j: (i, 0)),
        out_shape=jax.ShapeDtypeStruct((T, D), F32),
        scratch_shapes=[pltpu.VMEM((tm, D), BF16), pltpu.VMEM((tm, D), F32)],
        compiler_params=_params(("arbitrary", "arbitrary")),
        name="mlp_ln",
    )(h2, w1, w2, g, b)


def _tiles(T):
    pick = lambda pref: next(t for t in (pref, 512, 256, 128) if T % t == 0)
    return dict(tm_in=pick(1024), tn_in=896, tm_out=pick(512), tm_mlp=pick(512), tf_mlp=512)


def kernel(x, w_in, hg_lower_bounds, hg_norm_w, ssm_conv_w, ssm_conv_b, ssm_dt_bias, ssm_a_log, ssm_d, ssm_norm_w,
           w_out, ln1_g, ln1_b, w_mlp_in, w_mlp_out, ln2_g, ln2_b):
    B, S, D = x.shape
    assert D == D_MODEL and S % MOBA_BLOCK == 0 and S % CHUNK == 0
    T = B * S
    tl = _tiles(T)
    x2 = x.reshape(T, D).astype(F32)
    lbp = hg_lower_bounds.astype(F32)
    for l in range(DEPTH):
        wl = w_in[l]
        w_r = jnp.concatenate([wl[:, :SSM_DT_OFF], wl[:, ATT_OFF:], wl[:, SSM_DT_OFF:ATT_OFF],
                               jnp.zeros((D, LANES_V7X - SSM_HEADS), wl.dtype)], axis=1).astype(BF16)
        proj = _in_proj(x2, w_r, tl["tm_in"], tl["tn_in"])
        y_hg = _hgrn2(proj, lbp, hg_norm_w[l][None, :].astype(F32), l, B, S)
        y_ssm = _ssd(proj, ssm_conv_w[l].astype(F32), ssm_conv_b[l].astype(F32), ssm_dt_bias[l].astype(F32),
                     ssm_a_log[l].astype(F32), ssm_d[l].astype(F32), ssm_norm_w[l].astype(F32), B, S)
        y_att = _moba(proj, B, S)
        wo = w_out[l].astype(BF16)
        ws = [wo[:HG_W], wo[HG_W:HG_W + SSM_W], wo[HG_W + SSM_W:]]
        h2 = _out_proj([y_hg, y_ssm, y_att], ws, x2, ln1_g[l][None, :].astype(F32), ln1_b[l][None, :].astype(F32),
                       tl["tm_out"])
        x2 = _mlp(h2, w_mlp_in[l].astype(BF16), w_mlp_out[l].astype(BF16), ln2_g[l][None, :].astype(F32),
                  ln2_b[l][None, :].astype(F32), tl["tm_mlp"], tl["tf_mlp"])
    return x2.reshape(B, S, D).astype(x.dtype)
```
